```python
import math
import jax, jax.numpy as jnp
from jax import lax
import numpy as np

D_MODEL = 1024
BATCH = 4
SEQ = 4096
DEPTH = 4
DEC_BATCH = 128
DEC_SEQ = 1
PAST_LEN = 8192
PAGE_SIZE = 128

N_META = 16
N_EVEN = (DEPTH + 1) // 2
N_ODD = DEPTH // 2
QBLOCK = 128
SB_HEADS = 8
SB_HD = 64
SB_W = SB_HEADS * SB_HD
GDN_HEADS = 4
GDN_DK = 128
GDN_DV = 128
GDN_WK = GDN_HEADS * GDN_DK
GDN_WV = GDN_HEADS * GDN_DV
GDN_CONV = 4
GDN_CHUNK = 64
GDN_CONV_CH = 2 * GDN_WK + GDN_WV
EVEN_IN = 4 * SB_W + GDN_CONV_CH + GDN_WV + 2 * GDN_HEADS
EVEN_MIX = SB_W + GDN_WV
MLA_HEADS = 16
MLA_NOPE = 64
MLA_ROPE = 32
MLA_V = 64
MLA_QLORA = 384
MLA_KVLORA = 256
MLA_SCALE = (MLA_NOPE + MLA_ROPE) ** -0.5
ODD_IN = MLA_QLORA + MLA_KVLORA + MLA_ROPE + MLA_HEADS * MLA_V
ODD_MIX = MLA_HEADS * MLA_V
ROPE_THETA = 10000.0
NORM_EPS = 1e-6
DN_ALPHA = (2 * DEPTH) ** 0.25
DN_BETA = (8 * DEPTH) ** -0.25

kernel_name = 'hybrid_sb_gdn_mla_decoder_step'

F32 = jnp.float32


def split_cols(h, sizes):
    idx = [int(s) for s in np.cumsum(sizes)[:-1]]
    return jnp.split(h, idx, axis=-1)


def layer_norm(x, g, b):
    xf = x.astype(F32)
    mu = jnp.mean(xf, -1, keepdims=True)
    var = jnp.mean(jnp.square(xf - mu), -1, keepdims=True)
    return ((xf - mu) * lax.rsqrt(var + NORM_EPS) * g.astype(F32) + b.astype(F32)).astype(x.dtype)


def rms_norm(x, g):
    xf = x.astype(F32)
    return (xf * lax.rsqrt(jnp.mean(xf * xf, -1, keepdims=True) + NORM_EPS) * g.astype(F32)).astype(x.dtype)


def l2_normalize(x):
    xf = x.astype(F32)
    return xf * lax.rsqrt(jnp.sum(xf * xf, -1, keepdims=True) + NORM_EPS)


def rope(x, pos):
    half = MLA_ROPE // 2
    inv = ROPE_THETA ** (-jnp.arange(half, dtype=F32) / half)
    ang = pos.astype(F32)[:, None] * inv[None, :]
    shape = (1, pos.shape[0]) + (1,) * (x.ndim - 3) + (half,)
    cos = jnp.cos(ang).reshape(shape)
    sin = jnp.sin(ang).reshape(shape)
    xf = x.astype(F32)
    x1, x2 = xf[..., :half], xf[..., half:]
    return jnp.concatenate([x1 * cos - x2 * sin, x2 * cos + x1 * sin], -1).astype(x.dtype)


def sweep_query_blocks(attend, q_arrays, q_pos):
    T = q_pos.shape[0]
    if T <= QBLOCK:
        return attend(q_arrays, q_pos)
    nb = -(-T // QBLOCK)
    pad = nb * QBLOCK - T

    def blocks(a):
        a = jnp.pad(a, [(0, 0), (0, pad)] + [(0, 0)] * (a.ndim - 2))
        a = a.reshape((a.shape[0], nb, QBLOCK) + a.shape[2:])
        return jnp.moveaxis(a, 1, 0)

    qb = tuple(blocks(a) for a in q_arrays)
    pb = jnp.pad(q_pos, (0, pad), mode='edge').reshape(nb, QBLOCK)
    out = lax.map(lambda args: attend(args[0], args[1]), (qb, pb))
    out = jnp.moveaxis(out, 0, 1)
    out = out.reshape((out.shape[0], nb * QBLOCK) + out.shape[3:])
    return out[:, :T]


def stick_breaking_attend(q, k, v, q_pos, k_pos):
    z = jnp.einsum('bqhd,bkhd->bhqk', q.astype(F32), k.astype(F32)) * (SB_HD ** -0.5)
    mask = k_pos[None, :] < q_pos[:, None]
    log_not = jnp.where(mask, jax.nn.log_sigmoid(-z), 0.0)
    later = lax.cumsum(log_not, axis=3, reverse=True) - log_not
    w = jnp.where(mask, jnp.exp(jax.nn.log_sigmoid(z) + later), 0.0)
    return jnp.einsum('bhqk,bkhd->bqhd', w, v.astype(F32)).astype(q.dtype)


def mla_attend(q_lat, q_rope, ckv, kr, q_pos, k_pos):
    s = (jnp.einsum('bqhc,bkc->bhqk', q_lat.astype(F32), ckv.astype(F32))
         + jnp.einsum('bqhr,bkr->bhqk', q_rope.astype(F32), kr.astype(F32))) * MLA_SCALE
    s = jnp.where(k_pos[None, :] <= q_pos[:, None], s, -jnp.inf)
    p = jax.nn.softmax(s, axis=-1)
    return jnp.einsum('bhqk,bkc->bqhc', p, ckv.astype(F32)).astype(q_lat.dtype)


def causal_conv_silu(x_ext, w):
    C = w.shape[1]
    y = lax.conv_general_dilated(x_ext, w[:, None, :].astype(x_ext.dtype), window_strides=(1,),
                                 padding='VALID', dimension_numbers=('NWC', 'WIO', 'NWC'),
                                 feature_group_count=C)
    return jax.nn.silu(y)


def gdn_chunked(q, k, v, beta, g, s0, front_pad):
    B, T = q.shape[0], q.shape[1]
    back = (-(front_pad + T)) % GDN_CHUNK
    n = (front_pad + T + back) // GDN_CHUNK

    def chunks(a):
        a = jnp.pad(a.astype(F32), [(0, 0), (front_pad, back)] + [(0, 0)] * (a.ndim - 2))
        a = a.reshape((B, n, GDN_CHUNK) + a.shape[2:])
        return jnp.moveaxis(a, [1, 3], [0, 2])

    qc, kc, vc, bc, gc = (chunks(a) for a in (q, k, v, beta, g))
    G = jnp.cumsum(gc, axis=-1)
    idx = jnp.arange(GDN_CHUNK)
    incl = idx[:, None] >= idx[None, :]
    strict = idx[:, None] > idx[None, :]
    decay = jnp.exp(jnp.where(incl, G[..., :, None] - G[..., None, :], -jnp.inf))
    kb = kc * bc[..., None]
    a_mat = jnp.where(strict, jnp.einsum('nbhid,nbhjd->nbhij', kb, kc) * decay, 0.0)
    eye = jnp.eye(GDN_CHUNK, dtype=F32)
    t_mat = lax.linalg.triangular_solve(eye + a_mat, jnp.broadcast_to(eye, a_mat.shape),
                                        left_side=True, lower=True)
    w_mat = t_mat @ (kb * jnp.exp(G)[..., None])
    u_mat = t_mat @ (vc * bc[..., None])
    qk = jnp.einsum('nbhid,nbhjd->nbhij', qc, kc) * decay
    q_dec = qc * jnp.exp(G)[..., None]
    k_dec = kc * jnp.exp(G[..., -1:] - G)[..., None]
    g_last = jnp.exp(G[..., -1])

    def step(s, inp):
        q_d, k_d, w_c, u_c, qk_c, gl = inp
        v_new = u_c - w_c @ s
        o = q_d @ s + qk_c @ v_new
        s = s * gl[..., None, None] + jnp.einsum('bhck,bhcv->bhkv', k_d, v_new)
        return s, o

    s, o = lax.scan(step, s0.astype(F32), (q_dec, k_dec, w_mat, u_mat, qk, g_last))
    o = jnp.moveaxis(o, [0, 2], [1, 3]).reshape(B, n * GDN_CHUNK, GDN_HEADS, GDN_DV)
    return o[:, front_pad:front_pad + T], s


def even_mixer(x, pos, sb_k_past, sb_v_past, gdn_s0, conv_hist, front_pad,
               w_in, conv_w, a_log, dt_bias, norm_g, w_out):
    B, T, _ = x.shape
    q_a, k_a, v_a, gate_a, qkv_b, z_b, b_b, a_b = split_cols(
        x @ w_in, (SB_W, SB_W, SB_W, SB_W, GDN_CONV_CH, GDN_WV, GDN_HEADS, GDN_HEADS))
    q_a = q_a.reshape(B, T, SB_HEADS, SB_HD)
    k_a = k_a.reshape(B, T, SB_HEADS, SB_HD)
    v_a = v_a.reshape(B, T, SB_HEADS, SB_HD)
    k_all = jnp.concatenate([sb_k_past, k_a], axis=1)
    v_all = jnp.concatenate([sb_v_past, v_a], axis=1)
    k_pos = jnp.arange(k_all.shape[1], dtype=jnp.int32)
    o_a = sweep_query_blocks(lambda qs, qp: stick_breaking_attend(qs[0], k_all, v_all, qp, k_pos),
                             (q_a,), pos)
    o_a = o_a.reshape(B, T, SB_W) * jax.nn.silu(gate_a)
    conv_in = jnp.concatenate([conv_hist, qkv_b], axis=1)
    new_conv = conv_in[:, conv_in.shape[1] - (GDN_CONV - 1):]
    q_b, k_b, v_b = split_cols(causal_conv_silu(conv_in, conv_w), (GDN_WK, GDN_WK, GDN_WV))
    q_b = l2_normalize(q_b.reshape(B, T, GDN_HEADS, GDN_DK)) * (GDN_DK ** -0.5)
    k_b = l2_normalize(k_b.reshape(B, T, GDN_HEADS, GDN_DK))
    v_b = v_b.reshape(B, T, GDN_HEADS, GDN_DV).astype(F32)
    beta = jax.nn.sigmoid(b_b.astype(F32))
    g = -jnp.exp(a_log.astype(F32)) * jax.nn.softplus(a_b.astype(F32) + dt_bias.astype(F32))
    o_b, s_new = gdn_chunked(q_b, k_b, v_b, beta, g, gdn_s0, front_pad)
    o_b = rms_norm(o_b, norm_g) * jax.nn.silu(z_b.reshape(B, T, GDN_HEADS, GDN_DV).astype(F32))
    o_b = o_b.reshape(B, T, GDN_WV).astype(x.dtype)
    y = jnp.concatenate([o_a, o_b], axis=-1) @ w_out
    return y, k_a, v_a, s_new.astype(gdn_s0.dtype), new_conv


def odd_mixer(x, pos, ckv_past, kr_past, w_in, q_norm_g, kv_norm_g, w_uq, w_uk, w_uv, w_out):
    B, T, _ = x.shape
    c_q, c_kv, kr, gate = split_cols(x @ w_in, (MLA_QLORA, MLA_KVLORA, MLA_ROPE, ODD_MIX))
    q = (rms_norm(c_q, q_norm_g) @ w_uq).reshape(B, T, MLA_HEADS, MLA_NOPE + MLA_ROPE)
    q_nope = q[..., :MLA_NOPE]
    q_rope = rope(q[..., MLA_NOPE:], pos)
    c_kv = rms_norm(c_kv, kv_norm_g)
    kr = rope(kr, pos)
    q_lat = jnp.einsum('bthn,chn->bthc', q_nope, w_uk)
    ckv_all = jnp.concatenate([ckv_past, c_kv], axis=1)
    kr_all = jnp.concatenate([kr_past, kr], axis=1)
    k_pos = jnp.arange(ckv_all.shape[1], dtype=jnp.int32)
    o_lat = sweep_query_blocks(lambda qs, qp: mla_attend(qs[0], qs[1], ckv_all, kr_all, qp, k_pos),
                               (q_lat, q_rope), pos)
    o = jnp.einsum('bthc,chv->bthv', o_lat, w_uv).reshape(B, T, ODD_MIX) * jax.nn.silu(gate)
    return o @ w_out, c_kv, kr


def setup_inputs(seed: int = 0) -> dict:
    key = jax.random.key(seed)
    ks = jax.random.split(key, 32)
    n_pages = PAST_LEN // PAGE_SIZE
    n_pool = (DEC_BATCH * n_pages * 5) // 4

    def nrm(k, shape, scale=1.0):
        return jax.random.normal(k, shape, F32) * scale

    page_table = jax.random.permutation(ks[0], n_pool)[:DEC_BATCH * n_pages]
    page_table = page_table.reshape(DEC_BATCH, n_pages).astype(jnp.int32)
    dt = jnp.exp(jax.random.uniform(ks[1], (N_EVEN, GDN_HEADS), F32, math.log(1e-3), math.log(0.1)))
    return {
        'x_prompt': nrm(ks[2], (BATCH, SEQ, D_MODEL)),
        'x_sample': nrm(ks[3], (DEC_BATCH, DEC_SEQ, D_MODEL)),
        'cache_sb_k': nrm(ks[4], (N_EVEN, n_pool, PAGE_SIZE, SB_HEADS, SB_HD)),
        'cache_sb_v': nrm(ks[5], (N_EVEN, n_pool, PAGE_SIZE, SB_HEADS, SB_HD)),
        'state_gdn': nrm(ks[6], (N_EVEN, DEC_BATCH, GDN_HEADS, GDN_DK, GDN_DV), 0.5),
        'state_gdn_conv': nrm(ks[7], (N_EVEN, DEC_BATCH, GDN_CONV - 1, GDN_CONV_CH)),
        'cache_mla_ckv': nrm(ks[8], (N_ODD, n_pool, PAGE_SIZE, MLA_KVLORA)),
        'cache_mla_kr': nrm(ks[9], (N_ODD, n_pool, PAGE_SIZE, MLA_ROPE)),
        'page_table': page_table,
        'meta_tokens': nrm(ks[10], (N_META, D_MODEL)),
        'ln_g': 1.0 + nrm(ks[11], (DEPTH, D_MODEL), 0.02),
        'ln_b': nrm(ks[12], (DEPTH, D_MODEL), 0.02),
        'w_in_even': nrm(ks[13], (N_EVEN, D_MODEL, EVEN_IN), D_MODEL ** -0.5),
        'gdn_conv_w': nrm(ks[14], (N_EVEN, GDN_CONV, GDN_CONV_CH), GDN_CONV ** -0.5),
        'gdn_a_log': jnp.log(jax.random.uniform(ks[15], (N_EVEN, GDN_HEADS), F32, 1.0, 16.0)),
        'gdn_dt_bias': dt + jnp.log(-jnp.expm1(-dt)),
        'gdn_norm_g': 1.0 + nrm(ks[16], (N_EVEN, GDN_DV), 0.02),
        'w_out_even': nrm(ks[17], (N_EVEN, EVEN_MIX, D_MODEL), EVEN_MIX ** -0.5 * DN_BETA),
        'w_in_odd': nrm(ks[18], (N_ODD, D_MODEL, ODD_IN), D_MODEL ** -0.5),
        'mla_q_norm_g': 1.0 + nrm(ks[19], (N_ODD, MLA_QLORA), 0.02),
        'mla_kv_norm_g': 1.0 + nrm(ks[20], (N_ODD, MLA_KVLORA), 0.02),
        'mla_w_uq': nrm(ks[21], (N_ODD, MLA_QLORA, MLA_HEADS * (MLA_NOPE + MLA_ROPE)), MLA_QLORA ** -0.5),
        'mla_w_uk': nrm(ks[22], (N_ODD, MLA_KVLORA, MLA_HEADS, MLA_NOPE), MLA_KVLORA ** -0.5),
        'mla_w_uv': nrm(ks[23], (N_ODD, MLA_KVLORA, MLA_HEADS, MLA_V), MLA_KVLORA ** -0.5),
        'w_out_odd': nrm(ks[24], (N_ODD, ODD_MIX, D_MODEL), ODD_MIX ** -0.5 * DN_BETA),
    }


def reference(x_prompt, x_sample, cache_sb_k, cache_sb_v, state_gdn, state_gdn_conv,
              cache_mla_ckv, cache_mla_kr, page_table, meta_tokens, ln_g, ln_b,
              w_in_even, gdn_conv_w, gdn_a_log, gdn_dt_bias, gdn_norm_g, w_out_even,
              w_in_odd, mla_q_norm_g, mla_kv_norm_g, mla_w_uq, mla_w_uk, mla_w_uv, w_out_odd):
    b = x_prompt.shape[0]
    db, ds = x_sample.shape[0], x_sample.shape[1]
    past = page_table.shape[1] * PAGE_SIZE
    xp = jnp.concatenate([jnp.broadcast_to(meta_tokens.astype(x_prompt.dtype)[None], (b, N_META, D_MODEL)),
                          x_prompt], axis=1)
    xs = x_sample
    L = xp.shape[1]
    pos_p = jnp.arange(L, dtype=jnp.int32)
    pos_s = past + jnp.arange(ds, dtype=jnp.int32)
    front_pad = (-N_META) % GDN_CHUNK

    sbk_p, sbv_p, sbk_s, sbv_s = [], [], [], []
    gs_p, gs_s, gc_p, gc_s = [], [], [], []
    ck_p, ck_s, kr_p, kr_s = [], [], [], []
    for layer in range(DEPTH):
        if layer % 2 == 0:
            e = layer // 2
            prm = (w_in_even[e], gdn_conv_w[e], gdn_a_log[e], gdn_dt_bias[e], gdn_norm_g[e], w_out_even[e])
            yp, kp, vp, sp, cp = even_mixer(
                xp, pos_p,
                jnp.zeros((b, 0, SB_HEADS, SB_HD), xp.dtype), jnp.zeros((b, 0, SB_HEADS, SB_HD), xp.dtype),
                jnp.zeros((b, GDN_HEADS, GDN_DK, GDN_DV), state_gdn.dtype),
                jnp.zeros((b, GDN_CONV - 1, GDN_CONV_CH), xp.dtype), front_pad, *prm)
            k_past = cache_sb_k[e, page_table].reshape(db, past, SB_HEADS, SB_HD)
            v_past = cache_sb_v[e, page_table].reshape(db, past, SB_HEADS, SB_HD)
            ys, ks_, vs_, ss, cs = even_mixer(xs, pos_s, k_past, v_past, state_gdn[e], state_gdn_conv[e],
                                              0, *prm)
            sbk_p.append(kp); sbv_p.append(vp); sbk_s.append(ks_); sbv_s.append(vs_)
            gs_p.append(sp); gs_s.append(ss); gc_p.append(cp); gc_s.append(cs)
        else:
            o = layer // 2
            prm = (w_in_odd[o], mla_q_norm_g[o], mla_kv_norm_g[o], mla_w_uq[o], mla_w_uk[o], mla_w_uv[o],
                   w_out_odd[o])
            yp, ckp, krp = odd_mixer(xp, pos_p, jnp.zeros((b, 0, MLA_KVLORA), xp.dtype),
                                     jnp.zeros((b, 0, MLA_ROPE), xp.dtype), *prm)
            ckv_past = cache_mla_ckv[o, page_table].reshape(db, past, MLA_KVLORA)
            kr_past = cache_mla_kr[o, page_table].reshape(db, past, MLA_ROPE)
            ys, cks, krs = odd_mixer(xs, pos_s, ckv_past, kr_past, *prm)
            ck_p.append(ckp); ck_s.append(cks); kr_p.append(krp); kr_s.append(krs)
        xp = layer_norm(DN_ALPHA * xp + yp, ln_g[layer], ln_b[layer])
        xs = layer_norm(DN_ALPHA * xs + ys, ln_g[layer], ln_b[layer])

    return (xp[:, N_META:], xs,
            jnp.stack(sbk_p), jnp.stack(sbv_p), jnp.stack(sbk_s), jnp.stack(sbv_s),
            jnp.stack(gs_p), jnp.stack(gs_s), jnp.stack(gc_p), jnp.stack(gc_s),
            jnp.stack(ck_p), jnp.stack(ck_s), jnp.stack(kr_p), jnp.stack(kr_s))
```

```python
import functools

import jax
import jax.numpy as jnp
import numpy as np
from jax import lax
from jax.experimental import pallas as pl
from jax.experimental.pallas import tpu as pltpu

F32 = jnp.float32
BF16 = jnp.bfloat16

D_MODEL = 1024
DEPTH = 4
PAGE_SIZE = 128
N_META = 16
SB_HEADS = 8
SB_HD = 64
SB_W = SB_HEADS * SB_HD
GDN_HEADS = 4
GDN_DK = 128
GDN_DV = 128
GDN_WK = GDN_HEADS * GDN_DK
GDN_WV = GDN_HEADS * GDN_DV
GDN_CONV = 4
GDN_CONV_CH = 2 * GDN_WK + GDN_WV
MLA_HEADS = 16
MLA_NOPE = 64
MLA_ROPE = 32
MLA_V = 64
MLA_QLORA = 384
MLA_KVLORA = 256
MLA_SCALE = (MLA_NOPE + MLA_ROPE) ** -0.5
ODD_MIX = MLA_HEADS * MLA_V
ROPE_THETA = 10000.0
NORM_EPS = 1e-6
DN_ALPHA = (2 * DEPTH) ** 0.25

LANES = 128
ATT_BLOCK = 256
GDN_CHUNK = 64
ROW_TILE = 256
PAGES_PER_STEP = 8
VMEM_LIMIT = 56 * 1024 * 1024

EV_QKV = 0
EV_Q = GDN_CONV_CH
EV_GATE = EV_Q + SB_W
EV_Z = EV_GATE + SB_W
EV_BA = EV_Z + GDN_WV
EV_N = EV_BA + LANES
OD_CQ = 0
OD_CKV = MLA_QLORA
OD_GATE = OD_CKV + MLA_KVLORA
OD_KR = OD_GATE + ODD_MIX
OD_N = OD_KR + LANES

NT_DIMS = (((1,), (1,)), ((), ()))


def _cparams(sem):
    return pltpu.CompilerParams(dimension_semantics=sem, vmem_limit_bytes=VMEM_LIMIT)


def _sigmoid(x):
    return 1.0 / (1.0 + jnp.exp(-x))


def _silu(x):
    return x * _sigmoid(x)


def _softplus(x):
    return jnp.maximum(x, 0.0) + jnp.log1p(jnp.exp(-jnp.abs(x)))


def _dot(a, b):
    return jnp.dot(a, b, preferred_element_type=F32)


def _dot_nt(a, b):
    return lax.dot_general(a, b, NT_DIMS, preferred_element_type=F32)


def _dot_hi(a, b):
    return jnp.dot(a, b, preferred_element_type=F32, precision=lax.Precision.HIGHEST)


def _split3(x):
    hi = x.astype(BF16)
    r = x - hi.astype(F32)
    mid = r.astype(BF16)
    lo = (r - mid.astype(F32)).astype(BF16)
    return hi, mid, lo


def _linear_kernel(x_ref, w_ref, o_ref):
    o_ref[0] = _dot(x_ref[0].astype(BF16), w_ref[...])


def _linear_t_kernel(x_ref, w_ref, wt_ref, o_ref, ot_ref):
    x = x_ref[0].astype(BF16)
    o_ref[0] = _dot(x, w_ref[...])
    ot_ref[0] = _dot_nt(wt_ref[...], x)


def _linear(x, w, wt=None, *, tm):
    b, n, k = x.shape
    nm = w.shape[1]
    grid = (b, n // tm)
    x_spec = pl.BlockSpec((1, tm, k), lambda i, j: (i, j, 0))
    w_spec = pl.BlockSpec((k, nm), lambda i, j: (0, 0))
    o_spec = pl.BlockSpec((1, tm, nm), lambda i, j: (i, j, 0))
    if wt is None:
        return pl.pallas_call(
            _linear_kernel, grid=grid, in_specs=[x_spec, w_spec], out_specs=o_spec,
            out_shape=jax.ShapeDtypeStruct((b, n, nm), F32),
            compiler_params=_cparams(("parallel", "parallel")), name="linear")(x, w)
    nt = wt.shape[0]
    return pl.pallas_call(
        _linear_t_kernel, grid=grid,
        in_specs=[x_spec, w_spec, pl.BlockSpec((nt, k), lambda i, j: (0, 0))],
        out_specs=[o_spec, pl.BlockSpec((1, nt, tm), lambda i, j: (i, 0, j))],
        out_shape=[jax.ShapeDtypeStruct((b, n, nm), F32), jax.ShapeDtypeStruct((b, nt, n), F32)],
        compiler_params=_cparams(("parallel", "parallel")), name="linear_t")(x, w, wt)


def _out_ln_kernel(*refs, n_mix):
    mix_refs = refs[:n_mix]
    w_refs = refs[n_mix:2 * n_mix]
    x_ref, g_ref, b_ref, o_ref = refs[2 * n_mix:]
    y = _dot(mix_refs[0][0], w_refs[0][...])
    for m_ref, w_ref in zip(mix_refs[1:], w_refs[1:]):
        y = y + _dot(m_ref[0], w_ref[...])
    h = DN_ALPHA * x_ref[0] + y
    mu = jnp.mean(h, axis=-1, keepdims=True)
    hc = h - mu
    var = jnp.mean(hc * hc, axis=-1, keepdims=True)
    o_ref[0] = hc * lax.rsqrt(var + NORM_EPS) * g_ref[...] + b_ref[...]


def _out_ln(mixes, ws, x, g, b, *, tm):
    bb, n, d = x.shape
    grid = (bb, n // tm)
    in_specs = [pl.BlockSpec((1, tm, m.shape[2]), lambda i, j: (i, j, 0)) for m in mixes]
    in_specs += [pl.BlockSpec(w.shape, lambda i, j: (0, 0)) for w in ws]
    in_specs += [pl.BlockSpec((1, tm, d), lambda i, j: (i, j, 0)),
                 pl.BlockSpec((1, d), lambda i, j: (0, 0)),
                 pl.BlockSpec((1, d), lambda i, j: (0, 0))]
    return pl.pallas_call(
        functools.partial(_out_ln_kernel, n_mix=len(mixes)), grid=grid, in_specs=in_specs,
        out_specs=pl.BlockSpec((1, tm, d), lambda i, j: (i, j, 0)),
        out_shape=jax.ShapeDtypeStruct((bb, n, d), F32),
        compiler_params=_cparams(("parallel", "parallel")), name="out_ln")(*mixes, *ws, x, g, b)


def _sb_prompt_kernel(q_ref, gate_ref, kt_ref, vt_ref, o_ref, acc_ref, carry_ref):
    t = ATT_BLOCK
    qi = pl.program_id(2)
    lane = lax.broadcasted_iota(jnp.int32, (t, LANES), 1)
    q = q_ref[0] * (SB_HD ** -0.5)
    q_heads = (jnp.where(lane < SB_HD, q, 0.0).astype(BF16), jnp.where(lane < SB_HD, 0.0, q).astype(BF16))
    row = lax.broadcasted_iota(jnp.int32, (t, t), 0)
    col = lax.broadcasted_iota(jnp.int32, (t, t), 1)
    later_keys = (row > col).astype(BF16)
    causal = col < row
    first_rows = lax.broadcasted_iota(jnp.int32, (LANES, t), 0) < SB_HD

    acc_ref[...] = jnp.zeros_like(acc_ref)
    carry_ref[...] = jnp.zeros_like(carry_ref)

    def block(j, masked):
        start = pl.multiple_of(j * t, t)
        ktb = kt_ref[0, :, pl.ds(start, t)].astype(BF16)
        vt = vt_ref[0, :, pl.ds(start, t)]
        acc = acc_ref[...]
        for h in range(2):
            z = _dot(q_heads[h], ktb)
            sp = _softplus(z)
            ln = -sp
            if masked:
                ln = jnp.where(causal, ln, 0.0)
            hi = ln.astype(BF16)
            lo = (ln - hi.astype(F32)).astype(BF16)
            local = _dot(hi, later_keys) + _dot(lo, later_keys)
            carry = carry_ref[h]
            w = jnp.exp(z - sp + local + carry)
            if masked:
                w = jnp.where(causal, w, 0.0)
            vth = jnp.where(first_rows if h == 0 else jnp.logical_not(first_rows), vt, 0.0).astype(BF16)
            acc = acc + _dot_nt(w.astype(BF16), vth)
            carry_ref[h] = carry + local[:, 0:1] + ln[:, 0:1]
        acc_ref[...] = acc

    block(qi, True)

    def body(i, c):
        block(qi - 1 - i, False)
        return c

    lax.fori_loop(0, qi, body, 0)
    o_ref[0] = (acc_ref[...] * _silu(gate_ref[0])).astype(BF16)


def _sb_prompt(main, kvt):
    b, lp, _ = main.shape
    t = ATT_BLOCK
    pairs = SB_HEADS // 2
    q0 = EV_Q // LANES
    g0 = EV_GATE // LANES
    return pl.pallas_call(
        _sb_prompt_kernel, grid=(b, pairs, lp // t),
        in_specs=[pl.BlockSpec((1, t, LANES), lambda i, p, j: (i, j, q0 + p)),
                  pl.BlockSpec((1, t, LANES), lambda i, p, j: (i, j, g0 + p)),
                  pl.BlockSpec((1, LANES, lp), lambda i, p, j: (i, p, 0)),
                  pl.BlockSpec((1, LANES, lp), lambda i, p, j: (i, pairs + p, 0))],
        out_specs=pl.BlockSpec((1, t, LANES), lambda i, p, j: (i, j, p)),
        out_shape=jax.ShapeDtypeStruct((b, lp, SB_W), BF16),
        scratch_shapes=[pltpu.VMEM((t, LANES), F32), pltpu.VMEM((2, t, 1), F32)],
        compiler_params=_cparams(("parallel", "parallel", "arbitrary")), name="sb_prompt")(main, main, kvt, kvt)


def _l2n(x):
    return x * lax.rsqrt(jnp.sum(x * x, axis=-1, keepdims=True) + NORM_EPS)


def _gdn_gates(ba, prm):
    beta = _sigmoid(ba)
    g = -jnp.exp(prm[0:1]) * _softplus(ba + prm[1:2])
    return beta, g


def _gdn_prompt_kernel(qkv_ref, z_ref, ba_ref, cw_ref, prm_ref, ng_ref, o_ref, st_ref, s_ref, halo_ref, *, n_valid):
    c = GDN_CHUNK
    ci = pl.program_id(1)

    @pl.when(ci == 0)
    def _():
        s_ref[...] = jnp.zeros_like(s_ref)
        halo_ref[...] = jnp.zeros_like(halo_ref)

    x = qkv_ref[0]
    xe = jnp.concatenate([halo_ref[...], x], axis=0)
    cw = cw_ref[...]
    y = (cw[0:1] * xe[5:5 + c] + cw[1:2] * xe[6:6 + c] + cw[2:3] * xe[7:7 + c] + cw[3:4] * xe[8:8 + c])
    halo_ref[...] = x[c - 8:c]
    y = _silu(y)

    rows = ci * c + lax.broadcasted_iota(jnp.int32, (c, LANES), 0)
    valid = rows < n_valid
    beta_all, g_all = _gdn_gates(ba_ref[0], prm_ref[...])
    beta_all = jnp.where(valid, beta_all, 0.0)
    g_all = jnp.where(valid, g_all, 0.0)

    ri = lax.broadcasted_iota(jnp.int32, (c, c), 0)
    cj = lax.broadcasted_iota(jnp.int32, (c, c), 1)
    incl = ri >= cj
    strict = ri > cj
    tri_incl = incl.astype(F32)
    ones = jnp.ones((c, c), F32)
    eye = (ri == cj).astype(F32)
    z_all = z_ref[0]
    ng = ng_ref[...]

    for h in range(GDN_HEADS):
        q = _l2n(y[:, h * GDN_DK:(h + 1) * GDN_DK]) * (GDN_DK ** -0.5)
        k = _l2n(y[:, GDN_WK + h * GDN_DK:GDN_WK + (h + 1) * GDN_DK])
        v = y[:, 2 * GDN_WK + h * GDN_DV:2 * GDN_WK + (h + 1) * GDN_DV]
        beta = beta_all[:, h:h + 1]
        g = g_all[:, GDN_HEADS + h:GDN_HEADS + h + 1]
        g_col = _dot_hi(tri_incl, jnp.broadcast_to(g, (c, LANES)))
        g_row = _dot_hi(ones, jnp.where(ri <= cj, jnp.broadcast_to(g, (c, c)), 0.0))
        decay = jnp.exp(jnp.where(incl, g_col[:, :c] - g_row, -jnp.inf))
        e_g = jnp.exp(g_col)
        g_last = g_col[c - 1:c, :]
        kb = k * beta
        kbf = k.astype(BF16)
        a = jnp.where(strict, _dot_nt(kb.astype(BF16), kbf) * decay, 0.0)
        t_inv = eye - a
        p = _dot_hi(a, a)
        for it in range(5):
            t_inv = t_inv + _dot_hi(t_inv, p)
            if it < 4:
                p = _dot_hi(p, p)
        t_b = t_inv.astype(BF16)
        w = _dot(t_b, (kb * e_g).astype(BF16))
        u = _dot(t_b, (v * beta).astype(BF16))
        qk = _dot_nt(q.astype(BF16), kbf) * decay
        s = s_ref[h]
        s_b = s.astype(BF16)
        v_new = u - _dot(w.astype(BF16), s_b)
        o = _dot((q * e_g).astype(BF16), s_b) + _dot(qk.astype(BF16), v_new.astype(BF16))
        k_dec = k * jnp.exp(g_last - g_col)
        s_ref[h] = s * jnp.exp(g_last) + _dot(jnp.transpose(k_dec).astype(BF16), v_new.astype(BF16))
        o = o * lax.rsqrt(jnp.mean(o * o, axis=-1, keepdims=True) + NORM_EPS) * ng
        o = o * _silu(z_all[:, h * GDN_DV:(h + 1) * GDN_DV])
        o_ref[0, :, h * GDN_DV:(h + 1) * GDN_DV] = o.astype(BF16)

    @pl.when(ci == pl.num_programs(1) - 1)
    def _():
        st_ref[0] = s_ref[...]


def _gdn_prompt(main, conv_w, prm, norm_g, n_valid):
    b, lp, _ = main.shape
    c = GDN_CHUNK
    return pl.pallas_call(
        functools.partial(_gdn_prompt_kernel, n_valid=n_valid), grid=(b, lp // c),
        in_specs=[pl.BlockSpec((1, c, GDN_CONV_CH), lambda i, j: (i, j, 0)),
                  pl.BlockSpec((1, c, GDN_WV), lambda i, j: (i, j, EV_Z // GDN_WV)),
                  pl.BlockSpec((1, c, LANES), lambda i, j: (i, j, EV_BA // LANES)),
                  pl.BlockSpec((GDN_CONV, GDN_CONV_CH), lambda i, j: (0, 0)),
                  pl.BlockSpec((8, LANES), lambda i, j: (0, 0)),
                  pl.BlockSpec((1, GDN_DV), lambda i, j: (0, 0))],
        out_specs=[pl.BlockSpec((1, c, GDN_WV), lambda i, j: (i, j, 0)),
                   pl.BlockSpec((1, GDN_HEADS, GDN_DK, GDN_DV), lambda i, j: (i, 0, 0, 0))],
        out_shape=[jax.ShapeDtypeStruct((b, lp, GDN_WV), BF16),
                   jax.ShapeDtypeStruct((b, GDN_HEADS, GDN_DK, GDN_DV), F32)],
        scratch_shapes=[pltpu.VMEM((GDN_HEADS, GDN_DK, GDN_DV), F32), pltpu.VMEM((8, GDN_CONV_CH), F32)],
        compiler_params=_cparams(("parallel", "arbitrary")), name="gdn_prompt")(main, main, main, conv_w, prm, norm_g)


def _rms(x, g):
    return x * lax.rsqrt(jnp.mean(x * x, axis=-1, keepdims=True) + NORM_EPS) * g


def _rope128(x, tab):
    return (x * tab[:, 0:LANES] + pltpu.roll(x, LANES - MLA_ROPE // 2, 1) * tab[:, LANES:2 * LANES]
            + pltpu.roll(x, MLA_ROPE // 2, 1) * tab[:, 2 * LANES:3 * LANES])


def _mla_prep_kernel(m_ref, tab_ref, qg_ref, kg_ref, wuq_ref, wuk_ref, wuv_ref, *out_refs, decode):
    if decode:
        q_ref, ckv_ref, kr_ref, qlat_ref = out_refs
    else:
        q_ref, ckv_ref, kr_ref, k_ref, v_ref = out_refs
    tab = tab_ref[...]
    cq = _rms(m_ref[0, :, OD_CQ:OD_CQ + MLA_QLORA], qg_ref[...])
    ckv = _rms(m_ref[0, :, OD_CKV:OD_CKV + MLA_KVLORA], kg_ref[...])
    kr = _rope128(m_ref[0, :, OD_KR:OD_KR + LANES], tab)
    ckv_ref[0] = ckv
    kr_ref[0] = kr
    q = _dot(cq.astype(BF16), wuq_ref[...])
    ckv_b = ckv.astype(BF16)
    if not decode:
        kn = _dot(ckv_b, wuk_ref[...])
        v_ref[0] = _dot(ckv_b, wuv_ref[...]).astype(BF16)
    for h in range(MLA_HEADS):
        sl = slice(h * LANES, (h + 1) * LANES)
        qh = _rope128(q[:, sl], tab)
        q_ref[0, :, sl] = qh.astype(BF16)
        if decode:
            qlat_ref[h] = _dot(qh.astype(BF16), wuk_ref[h])
        else:
            k_ref[0, :, sl] = (kn[:, sl] + kr).astype(BF16)


def _mla_prep(main, tab, qg, kg, wuq, wuk, wuv, *, tm, decode):
    b, n, _ = main.shape
    grid = (b, n // tm)
    hw = MLA_HEADS * LANES
    row = lambda w: pl.BlockSpec((1, tm, w), lambda i, j: (i, j, 0))
    full = lambda a: pl.BlockSpec(a.shape, lambda i, j: (0,) * a.ndim)
    in_specs = [row(OD_N), pl.BlockSpec((tm, 3 * LANES), lambda i, j: (j, 0)), full(qg), full(kg),
                full(wuq), full(wuk), full(wuv)]
    out_specs = [row(hw), row(MLA_KVLORA), row(LANES)]
    out_shape = [jax.ShapeDtypeStruct((b, n, hw), BF16), jax.ShapeDtypeStruct((b, n, MLA_KVLORA), F32),
                 jax.ShapeDtypeStruct((b, n, LANES), F32)]
    if decode:
        out_specs.append(pl.BlockSpec((MLA_HEADS, tm, MLA_KVLORA), lambda i, j: (0, j, 0)))
        out_shape.append(jax.ShapeDtypeStruct((MLA_HEADS, n, MLA_KVLORA), F32))
    else:
        out_specs += [row(hw), row(ODD_MIX)]
        out_shape += [jax.ShapeDtypeStruct((b, n, hw), BF16), jax.ShapeDtypeStruct((b, n, ODD_MIX), BF16)]
    return pl.pallas_call(
        functools.partial(_mla_prep_kernel, decode=decode), grid=grid, in_specs=in_specs,
        out_specs=out_specs, out_shape=out_shape,
        compiler_params=_cparams(("parallel", "parallel")), name="mla_prep")(main, tab, qg, kg, wuq, wuk, wuv)


def _mla_prompt_kernel(q_ref, k_ref, v_ref, gate_ref, o_ref, acc_ref, m_ref, l_ref):
    t = ATT_BLOCK
    qi = pl.program_id(2)
    row = lax.broadcasted_iota(jnp.int32, (t, t), 0)
    col = lax.broadcasted_iota(jnp.int32, (t, t), 1)
    causal = col <= row

    def block(j, first):
        start = pl.multiple_of(j * t, t)
        vb = v_ref[0, pl.ds(start, t), :]
        for h in range(2):
            q = q_ref[0, :, h * LANES:(h + 1) * LANES]
            kb = k_ref[0, pl.ds(start, t), h * LANES:(h + 1) * LANES]
            s = _dot_nt(q, kb) * MLA_SCALE
            if first:
                s = jnp.where(causal, s, -jnp.inf)
                m_new = jnp.max(s, axis=1, keepdims=True)
                p = jnp.exp(s - m_new)
                l_ref[h] = jnp.sum(p, axis=1, keepdims=True)
                acc_ref[h] = _dot(p.astype(BF16), vb)
            else:
                m_old = m_ref[h]
                m_new = jnp.maximum(m_old, jnp.max(s, axis=1, keepdims=True))
                alpha = jnp.exp(m_old - m_new)
                p = jnp.exp(s - m_new)
                l_ref[h] = alpha * l_ref[h] + jnp.sum(p, axis=1, keepdims=True)
                acc_ref[h] = alpha * acc_ref[h] + _dot(p.astype(BF16), vb)
            m_ref[h] = m_new

    block(qi, True)

    def body(i, c):
        block(i, False)
        return c

    lax.fori_loop(0, qi, body, 0)
    lane = lax.broadcasted_iota(jnp.int32, (t, LANES), 1)
    o = jnp.where(lane < MLA_V, acc_ref[0] / l_ref[0], acc_ref[1] / l_ref[1])
    o_ref[0] = (o * _silu(gate_ref[0])).astype(BF16)


def _mla_prompt(q, k, v, main):
    b, lp, _ = q.shape
    t = ATT_BLOCK
    pairs = MLA_HEADS // 2
    g0 = OD_GATE // LANES
    return pl.pallas_call(
        _mla_prompt_kernel, grid=(b, pairs, lp // t),
        in_specs=[pl.BlockSpec((1, t, 2 * LANES), lambda i, p, j: (i, j, p)),
                  pl.BlockSpec((1, lp, 2 * LANES), lambda i, p, j: (i, 0, p)),
                  pl.BlockSpec((1, lp, LANES), lambda i, p, j: (i, 0, p)),
                  pl.BlockSpec((1, t, LANES), lambda i, p, j: (i, j, g0 + p))],
        out_specs=pl.BlockSpec((1, t, LANES), lambda i, p, j: (i, j, p)),
        out_shape=jax.ShapeDtypeStruct((b, lp, ODD_MIX), BF16),
        scratch_shapes=[pltpu.VMEM((2, t, LANES), F32), pltpu.VMEM((2, t, 1), F32), pltpu.VMEM((2, t, 1), F32)],
        compiler_params=_cparams(("parallel", "parallel", "arbitrary")), name="mla_prompt")(q, k, v, main)


def _sb_decode_kernel(pt_ref, q_ref, gate_ref, *refs):
    g = PAGES_PER_STEP
    k_refs = refs[:g]
    v_refs = refs[g:2 * g]
    o_ref, acc_ref, carry_ref = refs[2 * g:]
    s = pl.program_id(1)

    @pl.when(s == 0)
    def _():
        acc_ref[...] = jnp.zeros_like(acc_ref)
        carry_ref[...] = jnp.zeros_like(carry_ref)

    qb = q_ref[0].reshape(SB_HEADS, SB_HD, PAGE_SIZE)
    r = lax.broadcasted_iota(jnp.int32, (PAGE_SIZE, PAGE_SIZE), 0)
    c = lax.broadcasted_iota(jnp.int32, (PAGE_SIZE, PAGE_SIZE), 1)
    later_keys = (r > c).astype(BF16)
    carry = carry_ref[...]
    ws = []
    for i in range(g):
        z = jnp.sum(qb * k_refs[i][...], axis=1) * (SB_HD ** -0.5)
        sp = _softplus(z)
        ln = -sp
        hi, mid, lo = _split3(ln)
        local = _dot(hi, later_keys) + _dot(mid, later_keys) + _dot(lo, later_keys)
        ws.append(jnp.exp(z - sp + local + carry))
        carry = carry + local[:, 0:1] + ln[:, 0:1]
    carry_ref[...] = carry
    for h in range(SB_HEADS):
        a = acc_ref[h * SB_HD:(h + 1) * SB_HD, :]
        for i in range(g):
            a = a + ws[i][h:h + 1, :] * v_refs[i][h]
        acc_ref[h * SB_HD:(h + 1) * SB_HD, :] = a

    @pl.when(s == pl.num_programs(1) - 1)
    def _():
        tot = _dot_nt_hi(jnp.ones((8, PAGE_SIZE), F32), acc_ref[...])
        o_ref[0] = tot[0:1] * _silu(gate_ref[0])


def _dot_nt_hi(a, b):
    return lax.dot_general(a, b, NT_DIMS, preferred_element_type=F32, precision=lax.Precision.HIGHEST)


def _sb_decode(page_table_flat, q_bc, gate, kc, vc, layer, n_pages):
    db = q_bc.shape[0]
    g = PAGES_PER_STEP
    ns = n_pages // g

    def page_spec(i):
        def imap(b, s, pt):
            return (layer, pt[b * n_pages + (ns - 1 - s) * g + (g - 1 - i)], 0, 0, 0)
        return pl.BlockSpec((None, None, SB_HEADS, SB_HD, PAGE_SIZE), imap)

    in_specs = [pl.BlockSpec((1, SB_W, PAGE_SIZE), lambda b, s, pt: (b, 0, 0)),
                pl.BlockSpec((1, 1, SB_W), lambda b, s, pt: (b, 0, 0))]
    in_specs += [page_spec(i) for i in range(g)] * 2
    return pl.pallas_call(
        _sb_decode_kernel,
        grid_spec=pltpu.PrefetchScalarGridSpec(
            num_scalar_prefetch=1, grid=(db, ns), in_specs=in_specs,
            out_specs=pl.BlockSpec((1, 1, SB_W), lambda b, s, pt: (b, 0, 0)),
            scratch_shapes=[pltpu.VMEM((SB_W, PAGE_SIZE), F32), pltpu.VMEM((SB_HEADS, 1), F32)]),
        out_shape=jax.ShapeDtypeStruct((db, 1, SB_W), F32),
        compiler_params=_cparams(("parallel", "arbitrary")), name="sb_decode",
    )(page_table_flat, q_bc, gate, *([kc] * g), *([vc] * g))


def _gdn_decode_kernel(m_ref, hist_ref, st_ref, cw_ref, prm_ref, ng_ref, o_ref, nc_ref, ns_ref):
    bs = m_ref.shape[0]
    new = m_ref[:, EV_QKV:EV_QKV + GDN_CONV_CH]
    cw = cw_ref[...]
    y = cw[0:1] * hist_ref[0] + cw[1:2] * hist_ref[1] + cw[2:3] * hist_ref[2] + cw[3:4] * new
    nc_ref[0] = hist_ref[1]
    nc_ref[1] = hist_ref[2]
    nc_ref[2] = new
    y = _silu(y)
    beta_all, g_all = _gdn_gates(m_ref[:, EV_BA:EV_BA + LANES], prm_ref[...])
    ng = ng_ref[...]
    rows = lax.broadcasted_iota(jnp.int32, (8, GDN_DK), 0)
    for h in range(GDN_HEADS):
        q = _l2n(y[:, h * GDN_DK:(h + 1) * GDN_DK]) * (GDN_DK ** -0.5)
        k = _l2n(y[:, GDN_WK + h * GDN_DK:GDN_WK + (h + 1) * GDN_DK])
        v = y[:, 2 * GDN_WK + h * GDN_DV:2 * GDN_WK + (h + 1) * GDN_DV]
        beta = beta_all[:, h:h + 1]
        e_g = jnp.exp(g_all[:, GDN_HEADS + h:GDN_HEADS + h + 1])
        qk = jnp.sum(q * k, axis=-1, keepdims=True)
        o_rows = jnp.zeros((bs, GDN_DV), F32)
        for i in range(bs):
            s = st_ref[i, h]
            qi = jnp.broadcast_to(q[i:i + 1], (8, GDN_DK))
            ki = jnp.broadcast_to(k[i:i + 1], (8, GDN_DK))
            r = _dot(jnp.where(rows == 0, qi, ki).astype(BF16), s.astype(BF16))
            v_new = beta[i:i + 1] * (v[i:i + 1] - e_g[i:i + 1] * r[1:2])
            o_i = e_g[i:i + 1] * r[0:1] + qk[i:i + 1] * v_new
            k8 = jnp.where(rows == 0, ki, 0.0)
            v8 = jnp.where(rows == 0, jnp.broadcast_to(v_new, (8, GDN_DV)), 0.0)
            ns_ref[i, h] = s * e_g[i:i + 1] + _dot(jnp.transpose(k8).astype(BF16), v8.astype(BF16))
            o_rows = jnp.where(lax.broadcasted_iota(jnp.int32, (bs, GDN_DV), 0) == i,
                               jnp.broadcast_to(o_i, (bs, GDN_DV)), o_rows)
        o = o_rows * lax.rsqrt(jnp.mean(o_rows * o_rows, axis=-1, keepdims=True) + NORM_EPS) * ng
        o = o * _silu(m_ref[:, EV_Z + h * GDN_DV:EV_Z + (h + 1) * GDN_DV])
        o_ref[:, h * GDN_DV:(h + 1) * GDN_DV] = o.astype(BF16)


def _gdn_decode(main_s, hist, state, conv_w, prm, norm_g, layer):
    db = main_s.shape[0]
    bs = 16
    return pl.pallas_call(
        _gdn_decode_kernel, grid=(db // bs,),
        in_specs=[pl.BlockSpec((bs, EV_N), lambda i: (i, 0)),
                  pl.BlockSpec((None, GDN_CONV - 1, bs, GDN_CONV_CH), lambda i: (layer, 0, i, 0)),
                  pl.BlockSpec((None, bs, GDN_HEADS, GDN_DK, GDN_DV), lambda i: (layer, i, 0, 0, 0)),
                  pl.BlockSpec((GDN_CONV, GDN_CONV_CH), lambda i: (0, 0)),
                  pl.BlockSpec((8, LANES), lambda i: (0, 0)),
                  pl.BlockSpec((1, GDN_DV), lambda i: (0, 0))],
        out_specs=[pl.BlockSpec((bs, GDN_WV), lambda i: (i, 0)),
                   pl.BlockSpec((GDN_CONV - 1, bs, GDN_CONV_CH), lambda i: (0, i, 0)),
                   pl.BlockSpec((bs, GDN_HEADS, GDN_DK, GDN_DV), lambda i: (i, 0, 0, 0))],
        out_shape=[jax.ShapeDtypeStruct((db, GDN_WV), BF16),
                   jax.ShapeDtypeStruct((GDN_CONV - 1, db, GDN_CONV_CH), F32),
                   jax.ShapeDtypeStruct((db, GDN_HEADS, GDN_DK, GDN_DV), F32)],
        compiler_params=_cparams(("parallel",)), name="gdn_decode")(main_s, hist, state, conv_w, prm, norm_g)


def _mla_decode_kernel(pt_ref, ql_ref, qr_ref, ckv_own_ref, kr_own_ref, *refs):
    g = PAGES_PER_STEP
    c_refs = refs[:g]
    r_refs = refs[g:2 * g]
    o_ref, acc_ref, m_ref, l_ref = refs[2 * g:]
    s_idx = pl.program_id(1)
    ql = ql_ref[0]
    qr = qr_ref[0]

    @pl.when(s_idx == 0)
    def _():
        own = ckv_own_ref[0]
        s_own = (jnp.sum(ql.astype(F32) * own.astype(BF16).astype(F32), axis=1, keepdims=True)
                 + jnp.sum(qr.astype(F32) * kr_own_ref[0].astype(BF16).astype(F32), axis=1, keepdims=True)) * MLA_SCALE
        m_ref[...] = s_own
        l_ref[...] = jnp.ones_like(l_ref)
        acc_ref[...] = jnp.broadcast_to(own.astype(BF16).astype(F32), acc_ref.shape)

    pages = [c_refs[i][...].astype(BF16) for i in range(g)]
    scores = [(_dot_nt(ql, pages[i]) + _dot(qr, r_refs[i][...].astype(BF16))) * MLA_SCALE for i in range(g)]
    m_old = m_ref[...]
    m_new = m_old
    for sc in scores:
        m_new = jnp.maximum(m_new, jnp.max(sc, axis=1, keepdims=True))
    alpha = jnp.exp(m_old - m_new)
    l_new = alpha * l_ref[...]
    acc = alpha * acc_ref[...]
    for i in range(g):
        p = jnp.exp(scores[i] - m_new)
        l_new = l_new + jnp.sum(p, axis=1, keepdims=True)
        acc = acc + _dot(p.astype(BF16), pages[i])
    m_ref[...] = m_new
    l_ref[...] = l_new
    acc_ref[...] = acc

    @pl.when(s_idx == pl.num_programs(1) - 1)
    def _():
        o_ref[0] = acc / l_new


def _mla_decode(page_table_flat, qlat, qrope, ckv_own, kr_own, ckv_cache, krt_cache, layer, n_pages):
    db = qlat.shape[0]
    g = PAGES_PER_STEP
    ns = n_pages // g

    def page_spec(i, rows, cols):
        def imap(b, s, pt):
            return (layer, pt[b * n_pages + s * g + i], 0, 0)
        return pl.BlockSpec((None, None, rows, cols), imap)

    per_b = lambda r, c: pl.BlockSpec((1, r, c), lambda b, s, pt: (b, 0, 0))
    in_specs = [per_b(MLA_HEADS, MLA_KVLORA), per_b(MLA_HEADS, MLA_ROPE), per_b(1, MLA_KVLORA), per_b(1, MLA_ROPE)]
    in_specs += [page_spec(i, PAGE_SIZE, MLA_KVLORA) for i in range(g)]
    in_specs += [page_spec(i, MLA_ROPE, PAGE_SIZE) for i in range(g)]
    return pl.pallas_call(
        _mla_decode_kernel,
        grid_spec=pltpu.PrefetchScalarGridSpec(
            num_scalar_prefetch=1, grid=(db, ns), in_specs=in_specs,
            out_specs=per_b(MLA_HEADS, MLA_KVLORA),
            scratch_shapes=[pltpu.VMEM((MLA_HEADS, MLA_KVLORA), F32), pltpu.VMEM((MLA_HEADS, 1), F32),
                            pltpu.VMEM((MLA_HEADS, 1), F32)]),
        out_shape=jax.ShapeDtypeStruct((db, MLA_HEADS, MLA_KVLORA), F32),
        compiler_params=_cparams(("parallel", "arbitrary")), name="mla_decode",
    )(page_table_flat, qlat, qrope, ckv_own, kr_own, *([ckv_cache] * g), *([krt_cache] * g))


def _mla_decode_post_kernel(ol_ref, wuv_ref, gate_ref, o_ref):
    for p in range(MLA_HEADS // 2):
        o = (_dot(ol_ref[2 * p].astype(BF16), wuv_ref[2 * p]) + _dot(ol_ref[2 * p + 1].astype(BF16), wuv_ref[2 * p + 1]))
        sl = slice(p * LANES, (p + 1) * LANES)
        o_ref[0, :, sl] = (o * _silu(gate_ref[0, :, OD_GATE + p * LANES:OD_GATE + (p + 1) * LANES])).astype(BF16)


def _mla_decode_post(olat_hm, wuv_dec, main_s):
    db = olat_hm.shape[1]
    return pl.pallas_call(
        _mla_decode_post_kernel, grid=(1,),
        in_specs=[pl.BlockSpec(olat_hm.shape, lambda i: (0, 0, 0)),
                  pl.BlockSpec(wuv_dec.shape, lambda i: (0, 0, 0)),
                  pl.BlockSpec(main_s.shape, lambda i: (0, 0, 0))],
        out_specs=pl.BlockSpec((1, db, ODD_MIX), lambda i: (0, 0, 0)),
        out_shape=jax.ShapeDtypeStruct((1, db, ODD_MIX), BF16),
        compiler_params=_cparams(("arbitrary",)), name="mla_decode_post")(olat_hm, wuv_dec, main_s)


def _rope_table(pos):
    half = MLA_ROPE // 2
    inv = ROPE_THETA ** (-jnp.arange(half, dtype=F32) / half)
    ang = pos.astype(F32)[:, None] * inv[None, :]
    cos, sin = jnp.cos(ang), jnp.sin(ang)
    n = pos.shape[0]
    one = jnp.ones((n, MLA_NOPE), F32)
    zero = jnp.zeros((n, MLA_NOPE), F32)
    z16 = jnp.zeros((n, half), F32)
    c_pat = jnp.concatenate([one, cos, cos, jnp.ones((n, LANES - MLA_NOPE - MLA_ROPE), F32)], axis=1)
    s1 = jnp.concatenate([zero, -sin, z16, jnp.zeros((n, LANES - MLA_NOPE - MLA_ROPE), F32)], axis=1)
    s2 = jnp.concatenate([zero, z16, sin, jnp.zeros((n, LANES - MLA_NOPE - MLA_ROPE), F32)], axis=1)
    return jnp.concatenate([c_pat, s1, s2], axis=1)


def _even_weights(w_in, a_log, dt_bias):
    cols = np.cumsum([0, SB_W, SB_W, SB_W, SB_W, GDN_CONV_CH, GDN_WV, GDN_HEADS, GDN_HEADS])
    q_a, k_a, v_a, gate_a, qkv_b, z_b, b_b, a_b = (w_in[:, cols[i]:cols[i + 1]] for i in range(8))
    pad = jnp.zeros((D_MODEL, LANES - 2 * GDN_HEADS), w_in.dtype)
    w_main = jnp.concatenate([qkv_b, q_a, gate_a, z_b, b_b, a_b, pad], axis=1).astype(BF16)
    wt_prompt = jnp.concatenate([k_a, v_a], axis=1).T.astype(BF16)
    wt_decode = jnp.concatenate([k_a, v_a, q_a], axis=1).T.astype(BF16)
    prm = jnp.zeros((8, LANES), F32)
    prm = prm.at[0, GDN_HEADS:2 * GDN_HEADS].set(a_log).at[1, GDN_HEADS:2 * GDN_HEADS].set(dt_bias)
    return w_main, wt_prompt, wt_decode, prm


def _odd_weights(w_in, w_uq, w_uk, w_uv):
    c_q = w_in[:, :MLA_QLORA]
    c_kv = w_in[:, MLA_QLORA:MLA_QLORA + MLA_KVLORA]
    kr = w_in[:, MLA_QLORA + MLA_KVLORA:MLA_QLORA + MLA_KVLORA + MLA_ROPE]
    gate = w_in[:, MLA_QLORA + MLA_KVLORA + MLA_ROPE:]
    z = lambda n: jnp.zeros((D_MODEL, n), w_in.dtype)
    w_main = jnp.concatenate([c_q, c_kv, gate, z(MLA_NOPE), kr, z(LANES - MLA_NOPE - MLA_ROPE)], axis=1).astype(BF16)
    uq = w_uq.reshape(MLA_QLORA, MLA_HEADS, MLA_NOPE + MLA_ROPE)
    uq = jnp.pad(uq, ((0, 0), (0, 0), (0, LANES - MLA_NOPE - MLA_ROPE))).reshape(MLA_QLORA, MLA_HEADS * LANES).astype(BF16)
    ukc = jnp.pad(w_uk, ((0, 0), (0, 0), (0, LANES - MLA_NOPE))).reshape(MLA_KVLORA, MLA_HEADS * LANES).astype(BF16)
    uv = w_uv.reshape(MLA_KVLORA, ODD_MIX).astype(BF16)
    uk_dec = jnp.pad(jnp.transpose(w_uk, (1, 2, 0)), ((0, 0), (0, LANES - MLA_NOPE), (0, 0))).astype(BF16)
    uv_t = jnp.transpose(w_uv, (1, 0, 2))
    even = jnp.pad(uv_t, ((0, 0), (0, 0), (0, MLA_V)))
    odd = jnp.pad(uv_t, ((0, 0), (0, 0), (MLA_V, 0)))
    uv_dec = jnp.where((jnp.arange(MLA_HEADS) % 2 == 0)[:, None, None], even, odd).astype(BF16)
    return w_main, uq, ukc, uv, uk_dec, uv_dec


def kernel(x_prompt, x_sample, cache_sb_k, cache_sb_v, state_gdn, state_gdn_conv, cache_mla_ckv, cache_mla_kr, page_table, meta_tokens, ln_g, ln_b, w_in_even, gdn_conv_w, gdn_a_log, gdn_dt_bias, gdn_norm_g, w_out_even, w_in_odd, mla_q_norm_g, mla_kv_norm_g, mla_w_uq, mla_w_uk, mla_w_uv, w_out_odd):
    b, seq, d = x_prompt.shape
    db = x_sample.shape[0]
    assert d == D_MODEL and x_sample.shape[1] == 1
    n_pages = page_table.shape[1]
    assert n_pages % PAGES_PER_STEP == 0 and db % 16 == 0
    past = n_pages * PAGE_SIZE
    l = N_META + seq
    lp = -(-l // ATT_BLOCK) * ATT_BLOCK
    tm = ROW_TILE

    xp = jnp.concatenate([jnp.broadcast_to(meta_tokens.astype(x_prompt.dtype)[None], (b, N_META, d)), x_prompt,
                          jnp.zeros((b, lp - l, d), x_prompt.dtype)], axis=1)
    xs = x_sample.reshape(1, db, d)
    pt_flat = page_table.reshape(-1).astype(jnp.int32)
    kc = jnp.transpose(cache_sb_k, (0, 1, 3, 4, 2))
    vc = jnp.transpose(cache_sb_v, (0, 1, 3, 4, 2))
    krt = jnp.transpose(cache_mla_kr, (0, 1, 3, 2))
    hist = jnp.transpose(state_gdn_conv, (0, 2, 1, 3))
    tab_p = _rope_table(jnp.arange(lp, dtype=jnp.int32))
    tab_s = _rope_table(jnp.full((db,), past, jnp.int32))

    sbk_p, sbv_p, sbk_s, sbv_s = [], [], [], []
    gs_p, gs_s, gc_p, gc_s = [], [], [], []
    ck_p, ck_s, kr_p, kr_s = [], [], [], []
    for layer in range(DEPTH):
        g_row = ln_g[layer].reshape(1, d)
        b_row = ln_b[layer].reshape(1, d)
        if layer % 2 == 0:
            e = layer // 2
            w_main, wt_p, wt_s, prm = _even_weights(w_in_even[e], gdn_a_log[e], gdn_dt_bias[e])
            w_out = w_out_even[e].astype(BF16)
            ng = gdn_norm_g[e].reshape(1, GDN_DV)
            main, kvt = _linear(xp, w_main, wt_p, tm=tm)
            mix_a = _sb_prompt(main, kvt)
            mix_b, st_p = _gdn_prompt(main, gdn_conv_w[e], prm, ng, l)
            xp = _out_ln([mix_a, mix_b], [w_out[:SB_W], w_out[SB_W:]], xp, g_row, b_row, tm=tm)
            sbk_p.append(jnp.transpose(kvt[:, :SB_W, :l].reshape(b, SB_HEADS, SB_HD, l), (0, 3, 1, 2)))
            sbv_p.append(jnp.transpose(kvt[:, SB_W:, :l].reshape(b, SB_HEADS, SB_HD, l), (0, 3, 1, 2)))
            gs_p.append(st_p)
            gc_p.append(main[:, l - (GDN_CONV - 1):l, :GDN_CONV_CH])
            main_s, kvqt = _linear(xs, w_main, wt_s, tm=db)
            q_bc = jnp.broadcast_to(jnp.transpose(kvqt[0, 2 * SB_W:, :])[:, :, None], (db, SB_W, PAGE_SIZE))
            gate_s = main_s[0, :, EV_GATE:EV_GATE + SB_W].reshape(db, 1, SB_W)
            mix_as = _sb_decode(pt_flat, q_bc, gate_s, kc, vc, e, n_pages)
            mix_bs, nconv, nstate = _gdn_decode(main_s[0], hist, state_gdn, gdn_conv_w[e], prm, ng, e)
            xs = _out_ln([mix_as.reshape(1, db, SB_W).astype(BF16), mix_bs.reshape(1, db, GDN_WV)],
                         [w_out[:SB_W], w_out[SB_W:]], xs, g_row, b_row, tm=db)
            sbk_s.append(jnp.transpose(kvqt[0, :SB_W].reshape(SB_HEADS, SB_HD, db), (2, 0, 1))[:, None])
            sbv_s.append(jnp.transpose(kvqt[0, SB_W:2 * SB_W].reshape(SB_HEADS, SB_HD, db), (2, 0, 1))[:, None])
            gs_s.append(nstate)
            gc_s.append(jnp.transpose(nconv, (1, 0, 2)))
        else:
            o = layer // 2
            w_main, uq, ukc, uv, uk_dec, uv_dec = _odd_weights(w_in_odd[o], mla_w_uq[o], mla_w_uk[o], mla_w_uv[o])
            w_out = w_out_odd[o].astype(BF16)
            qg = mla_q_norm_g[o].reshape(1, MLA_QLORA)
            kg = mla_kv_norm_g[o].reshape(1, MLA_KVLORA)
            main = _linear(xp, w_main, tm=tm)
            q, ckv, kr, k, v = _mla_prep(main, tab_p, qg, kg, uq, ukc, uv, tm=tm, decode=False)
            mix = _mla_prompt(q, k, v, main)
            xp = _out_ln([mix], [w_out], xp, g_row, b_row, tm=tm)
            ck_p.append(ckv[:, :l])
            kr_p.append(kr[:, :l, MLA_NOPE:MLA_NOPE + MLA_ROPE])
            main_s = _linear(xs, w_main, tm=db)
            q_s, ckv_s, kr_s_, qlat = _mla_prep(main_s, tab_s, qg, kg, uq, uk_dec, uv, tm=db, decode=True)
            qlat_b = jnp.transpose(qlat, (1, 0, 2)).astype(BF16)
            qrope = q_s[0].reshape(db, MLA_HEADS, LANES)[:, :, MLA_NOPE:MLA_NOPE + MLA_ROPE]
            kr_own = kr_s_[0, :, MLA_NOPE:MLA_NOPE + MLA_ROPE].reshape(db, 1, MLA_ROPE)
            olat = _mla_decode(pt_flat, qlat_b, qrope, ckv_s.reshape(db, 1, MLA_KVLORA), kr_own,
                               cache_mla_ckv, krt, o, n_pages)
            mix_s = _mla_decode_post(jnp.transpose(olat, (1, 0, 2)), uv_dec, main_s)
            xs = _out_ln([mix_s], [w_out], xs, g_row, b_row, tm=db)
            ck_s.append(ckv_s.reshape(db, 1, MLA_KVLORA))
            kr_s.append(kr_own)

    return (xp[:, N_META:l], xs.reshape(db, 1, d),
            jnp.stack(sbk_p), jnp.stack(sbv_p), jnp.stack(sbk_s), jnp.stack(sbv_s),
            jnp.stack(gs_p), jnp.stack(gs_s), jnp.stack(gc_p), jnp.stack(gc_s),
            jnp.stack(ck_p), jnp.stack(ck_s), jnp.stack(kr_p), jnp.stack(kr_s))
```
